```python
import jax, jax.numpy as jnp
from jax import lax
import numpy as np

D_MODEL = 2048
BATCH = 1
SEQ = 8192
DEPTH = 4

N_MIXERS = 2
N_A_LAYERS = (DEPTH + 1) // 2
N_B_LAYERS = DEPTH // 2
EPS = 1e-6

EXPAND = 2
CHUNK = 128
A_WIDTH = EXPAND * D_MODEL
A_GROUPS = 16
A_GROUP_DIM = A_WIDTH // A_GROUPS

B_HEAD_DIM = 128
B_HEADS = D_MODEL // B_HEAD_DIM
B_WIDTH = B_HEADS * B_HEAD_DIM
Q_BLOCK = 128
FORGET_BIAS_MEAN = 3.0

kernel_name = "hybrid_gmlp_fox_interleaved"


def rms_norm(x, g):
    xf = x.astype(jnp.float32)
    y = xf * lax.rsqrt(jnp.mean(xf * xf, axis=-1, keepdims=True) + EPS)
    return (y * g.astype(jnp.float32)).astype(x.dtype)


def spatial_gating_layer(x, norm_g, w_in, v_norm_g, w_s, b_s, w_out):
    b, s, _ = x.shape
    h = rms_norm(x, norm_g)
    u, v, z = jnp.split(h @ w_in, 3, axis=-1)
    u = jax.nn.gelu(u)
    v = rms_norm(jax.nn.gelu(v), v_norm_g)
    n_chunks = s // CHUNK
    v = v.reshape(b, n_chunks, CHUNK, A_GROUPS, A_GROUP_DIM)
    causal = jnp.tril(jnp.ones((CHUNK, CHUNK), dtype=bool))
    w_causal = jnp.where(causal[None], w_s, 0)
    mixed = jnp.einsum('gts,bcsgd->bctgd', w_causal, v) + b_s.T[None, None, :, :, None]
    mixed = mixed.reshape(b, s, A_WIDTH)
    y = u * mixed * jax.nn.silu(z)
    return y @ w_out


def forgetting_attention_layer(x, norm_g, w_in, f_bias, q_norm_g, k_norm_g, w_out):
    b, s, _ = x.shape
    h = rms_norm(x, norm_g)
    proj = h @ w_in
    q, k, v, z, f_logit = jnp.split(
        proj, [B_WIDTH, 2 * B_WIDTH, 3 * B_WIDTH, 4 * B_WIDTH], axis=-1)
    q = rms_norm(q.reshape(b, s, B_HEADS, B_HEAD_DIM), q_norm_g)
    k = rms_norm(k.reshape(b, s, B_HEADS, B_HEAD_DIM), k_norm_g)
    v = v.reshape(b, s, B_HEADS, B_HEAD_DIM)
    log_f = jax.nn.log_sigmoid((f_logit + f_bias).astype(jnp.float32))
    cum = jnp.cumsum(log_f, axis=1).transpose(0, 2, 1)
    scale = B_HEAD_DIM ** -0.5
    q_t = q.transpose(0, 2, 1, 3)
    k_t = k.transpose(0, 2, 1, 3)
    v_t = v.transpose(0, 2, 1, 3)
    n_blocks = s // Q_BLOCK
    q_blocks = q_t.reshape(b, B_HEADS, n_blocks, Q_BLOCK, B_HEAD_DIM).transpose(2, 0, 1, 3, 4)
    c_blocks = cum.reshape(b, B_HEADS, n_blocks, Q_BLOCK).transpose(2, 0, 1, 3)
    blk_idx = jnp.arange(n_blocks, dtype=jnp.int32)
    k_pos = jnp.arange(s, dtype=jnp.int32)

    def attend_block(args):
        q_blk, c_blk, i = args
        logits = jnp.einsum('bhqd,bhkd->bhqk', q_blk, k_t,
                            preferred_element_type=jnp.float32) * scale
        logits = logits + c_blk[..., :, None] - cum[:, :, None, :]
        q_pos = i * Q_BLOCK + jnp.arange(Q_BLOCK, dtype=jnp.int32)
        mask = k_pos[None, :] <= q_pos[:, None]
        logits = jnp.where(mask, logits, -jnp.inf)
        p = jax.nn.softmax(logits, axis=-1).astype(v_t.dtype)
        return jnp.einsum('bhqk,bhkd->bhqd', p, v_t)

    out = lax.map(attend_block, (q_blocks, c_blocks, blk_idx))
    out = out.transpose(1, 0, 3, 2, 4).reshape(b, s, B_WIDTH)
    y = out * jax.nn.silu(z)
    return y @ w_out


def setup_inputs(seed: int = 0) -> dict:
    key = jax.random.key(seed)
    ks = jax.random.split(key, 13)
    f32 = jnp.float32
    x = jax.random.normal(ks[0], (BATCH, SEQ, D_MODEL), f32)
    a_norm_g = 1.0 + 0.05 * jax.random.normal(ks[1], (N_A_LAYERS, D_MODEL), f32)
    a_w_in = jax.random.normal(ks[2], (N_A_LAYERS, D_MODEL, 3 * A_WIDTH), f32) * D_MODEL ** -0.5
    a_v_norm_g = 1.0 + 0.05 * jax.random.normal(ks[3], (N_A_LAYERS, A_WIDTH), f32)
    a_w_s = jax.random.normal(ks[4], (N_A_LAYERS, A_GROUPS, CHUNK, CHUNK), f32) * CHUNK ** -0.5
    a_b_s = 1.0 + 0.1 * jax.random.normal(ks[5], (N_A_LAYERS, A_GROUPS, CHUNK), f32)
    a_w_out = jax.random.normal(ks[6], (N_A_LAYERS, A_WIDTH, D_MODEL), f32) * A_WIDTH ** -0.5
    b_norm_g = 1.0 + 0.05 * jax.random.normal(ks[7], (N_B_LAYERS, D_MODEL), f32)
    b_w_in = jax.random.normal(ks[8], (N_B_LAYERS, D_MODEL, 4 * B_WIDTH + B_HEADS), f32) * D_MODEL ** -0.5
    b_f_bias = FORGET_BIAS_MEAN + 0.5 * jax.random.normal(ks[9], (N_B_LAYERS, B_HEADS), f32)
    b_q_norm_g = 1.0 + 0.05 * jax.random.normal(ks[10], (N_B_LAYERS, B_HEAD_DIM), f32)
    b_k_norm_g = 1.0 + 0.05 * jax.random.normal(ks[11], (N_B_LAYERS, B_HEAD_DIM), f32)
    b_w_out = jax.random.normal(ks[12], (N_B_LAYERS, B_WIDTH, D_MODEL), f32) * B_WIDTH ** -0.5
    return {"x": x, "a_norm_g": a_norm_g, "a_w_in": a_w_in, "a_v_norm_g": a_v_norm_g,
            "a_w_s": a_w_s, "a_b_s": a_b_s, "a_w_out": a_w_out,
            "b_norm_g": b_norm_g, "b_w_in": b_w_in, "b_f_bias": b_f_bias,
            "b_q_norm_g": b_q_norm_g, "b_k_norm_g": b_k_norm_g, "b_w_out": b_w_out}


def reference(x, a_norm_g, a_w_in, a_v_norm_g, a_w_s, a_b_s, a_w_out,
              b_norm_g, b_w_in, b_f_bias, b_q_norm_g, b_k_norm_g, b_w_out):
    for i in range(DEPTH):
        j = i // N_MIXERS
        if i % N_MIXERS == 0:
            x = x + spatial_gating_layer(x, a_norm_g[j], a_w_in[j], a_v_norm_g[j],
                                         a_w_s[j], a_b_s[j], a_w_out[j])
        else:
            x = x + forgetting_attention_layer(x, b_norm_g[j], b_w_in[j], b_f_bias[j],
                                               b_q_norm_g[j], b_k_norm_g[j], b_w_out[j])
    return x
```

```python
import math

import jax
import jax.numpy as jnp
from jax import lax
from jax.experimental import pallas as pl
from jax.experimental.pallas import tpu as pltpu

F32 = jnp.float32
BF16 = jnp.bfloat16

EPS = 1e-6
LANES = 128
CHUNK = 128
N_GROUPS = 16
HEAD_DIM = 128
QK_WIDTH = 2 * HEAD_DIM
VMEM_LIMIT = 56 * 1024 * 1024

A_TM = 512
A_BV = 1024
A_GS = 2

B_TM = 512
B_NB = 1024
B_TQ = 512
O_TM = 512


def _gelu_tanh(x):
    c = math.sqrt(2.0 / math.pi)
    return x * (0.5 * (1.0 + jnp.tanh(c * (x + 0.044715 * (x * x * x)))))


def _silu(x):
    return x / (1.0 + jnp.exp(-x))


def _log_sigmoid(x):
    return -(jnp.maximum(-x, 0.0) + jnp.log1p(jnp.exp(-jnp.abs(x))))


def _rms_scale(x, width):
    return lax.rsqrt(jnp.sum(x * x, axis=-1, keepdims=True) * (1.0 / width) + EPS)


def _split3(x):
    a1 = x.astype(BF16)
    r1 = x - a1.astype(F32)
    a2 = r1.astype(BF16)
    a3 = (r1 - a2.astype(F32)).astype(BF16)
    return a1, a2, a3


def _gmlp_kernel(x_ref, ng_ref, wv_ref, wu_ref, wz_ref, vg_ref, ws_ref, bs_ref, wo_ref,
                 o_ref, h_s, gv_s, ssq_s, r_s, *, nv, width):
    j = pl.program_id(1)
    tm = x_ref.shape[0]
    gdim = gv_s.shape[2]

    @pl.when(j == 0)
    def _():
        x = x_ref[...]
        h_s[...] = (x * _rms_scale(x, x.shape[1]) * ng_ref[...]).astype(BF16)
        ssq_s[...] = jnp.zeros_like(ssq_s)

    @pl.when(j < nv)
    def _():
        v = jnp.dot(h_s[...], wv_ref[...], preferred_element_type=F32)
        gv = _gelu_tanh(v)
        sq = gv * gv
        part = sq[:, 0:LANES]
        for c in range(1, gv.shape[1] // LANES):
            part = part + sq[:, c * LANES:(c + 1) * LANES]
        ssq_s[...] += part
        per_step = gv.shape[1] // gdim
        for t in range(per_step):
            gv_s[j * per_step + t] = gv[:, t * gdim:(t + 1) * gdim].astype(BF16)

    @pl.when(j >= nv)
    def _():
        jj = j - nv

        @pl.when(jj == 0)
        def _():
            r_s[...] = lax.rsqrt(jnp.sum(ssq_s[...], axis=1, keepdims=True) * (1.0 / width) + EPS)

        h = h_s[...]
        u = jnp.dot(h, wu_ref[...], preferred_element_type=F32)
        z = jnp.dot(h, wz_ref[...], preferred_element_type=F32)
        r = r_s[...]
        gs = ws_ref.shape[0]
        row = lax.broadcasted_iota(jnp.int32, (CHUNK, CHUNK), 0)
        col = lax.broadcasted_iota(jnp.int32, (CHUNK, CHUNK), 1)
        causal = row >= col
        ys = []
        for t in range(gs):
            sl = slice(t * gdim, (t + 1) * gdim)
            vn = (gv_s[jj * gs + t].astype(F32) * r * vg_ref[:, sl]).astype(BF16)
            w = jnp.where(causal, ws_ref[t], 0.0).astype(BF16)
            b = bs_ref[t]
            mixed = jnp.concatenate(
                [jnp.dot(w, vn[c * CHUNK:(c + 1) * CHUNK], preferred_element_type=F32) + b
                 for c in range(tm // CHUNK)], axis=0)
            ys.append((_gelu_tanh(u[:, sl]) * mixed * _silu(z[:, sl])).astype(BF16))
        y = jnp.concatenate(ys, axis=1)
        contrib = jnp.dot(y, wo_ref[...], preferred_element_type=F32)

        @pl.when(jj == 0)
        def _():
            o_ref[...] = x_ref[...] + contrib

        @pl.when(jj > 0)
        def _():
            o_ref[...] += contrib


def _gmlp_layer(x, norm_g, w_in, v_norm_g, w_s, b_s, w_out):
    s, d = x.shape
    width = w_out.shape[0]
    gdim = width // N_GROUPS
    tm, bv, gs = A_TM, A_BV, A_GS
    bu = gs * gdim
    nv, nu = width // bv, N_GROUPS // gs

    def uz(j):
        return jnp.maximum(j - nv, 0)

    in_specs = [
        pl.BlockSpec((tm, d), lambda i, j: (i, 0)),
        pl.BlockSpec((1, d), lambda i, j: (0, 0)),
        pl.BlockSpec((d, bv), lambda i, j: (0, width // bv + jnp.minimum(j, nv - 1))),
        pl.BlockSpec((d, bu), lambda i, j: (0, uz(j))),
        pl.BlockSpec((d, bu), lambda i, j: (0, 2 * width // bu + uz(j))),
        pl.BlockSpec((1, bu), lambda i, j: (0, uz(j))),
        pl.BlockSpec((gs, CHUNK, CHUNK), lambda i, j: (uz(j), 0, 0)),
        pl.BlockSpec((gs, CHUNK, 1), lambda i, j: (uz(j), 0, 0)),
        pl.BlockSpec((bu, d), lambda i, j: (uz(j), 0)),
    ]
    w_in_b = w_in.astype(BF16)
    return pl.pallas_call(
        lambda *refs: _gmlp_kernel(*refs, nv=nv, width=width),
        grid=(s // tm, nv + nu),
        in_specs=in_specs,
        out_specs=pl.BlockSpec((tm, d), lambda i, j: (i, 0)),
        out_shape=jax.ShapeDtypeStruct((s, d), F32),
        scratch_shapes=[
            pltpu.VMEM((tm, d), BF16),
            pltpu.VMEM((N_GROUPS, tm, gdim), BF16),
            pltpu.VMEM((tm, LANES), F32),
            pltpu.VMEM((tm, 1), F32),
        ],
        compiler_params=pltpu.CompilerParams(
            dimension_semantics=("arbitrary", "arbitrary"), vmem_limit_bytes=VMEM_LIMIT),
        name="gmlp_layer",
    )(x, norm_g.reshape(1, d), w_in_b, w_in_b, w_in_b, v_norm_g.reshape(1, width),
      w_s, b_s.reshape(N_GROUPS, CHUNK, 1), w_out.astype(BF16))


def _fox_proj_kernel(x_ref, ng_ref, wf_ref, fb_ref, w_ref, qg_ref, kg_ref,
                     q_ref, k_ref, vt_ref, z_ref, h_s, augq_s, augk_s, carry_s, *, n_heads, per_kind):
    i = pl.program_id(0)
    j = pl.program_id(1)
    tm = x_ref.shape[0]
    hps = w_ref.shape[1] // HEAD_DIM

    @pl.when(j == 0)
    def _():
        @pl.when(i == 0)
        def _():
            carry_s[...] = jnp.zeros_like(carry_s)

        x = x_ref[...]
        h = (x * _rms_scale(x, x.shape[1]) * ng_ref[...]).astype(BF16)
        h_s[...] = h
        logit = jnp.dot(h, wf_ref[...], preferred_element_type=F32) + fb_ref[...]
        a1, a2, a3 = _split3(_log_sigmoid(logit))
        row = lax.broadcasted_iota(jnp.int32, (tm, tm), 0)
        col = lax.broadcasted_iota(jnp.int32, (tm, tm), 1)
        tri = jnp.where(row >= col, 1.0, 0.0).astype(BF16)
        cum = (jnp.dot(tri, a1, preferred_element_type=F32)
               + jnp.dot(tri, a2, preferred_element_type=F32)
               + jnp.dot(tri, a3, preferred_element_type=F32)) + carry_s[...]
        carry_s[...] = cum[tm - 1:tm, :]
        lane = lax.broadcasted_iota(jnp.int32, (tm, LANES), 1)
        one = jnp.ones((tm, LANES), F32)
        zero = jnp.zeros((tm, LANES), F32)
        for hh in range(n_heads):
            c1, c2, c3 = _split3(jnp.broadcast_to(cum[:, hh:hh + 1], (tm, LANES)))
            c1, c2, c3 = c1.astype(F32), c2.astype(F32), c3.astype(F32)
            augq = jnp.where(lane == 0, c1, jnp.where(lane == 1, c2, jnp.where(
                lane == 2, c3, jnp.where(lane < 6, one, zero))))
            augk = jnp.where(lane < 3, one, jnp.where(lane == 3, -c1, jnp.where(
                lane == 4, -c2, jnp.where(lane == 5, -c3, zero))))
            augq_s[hh] = augq.astype(BF16)
            augk_s[hh] = augk.astype(BF16)

    res = jnp.dot(h_s[...], w_ref[...], preferred_element_type=F32)

    def qk_norm_store(out_ref, gain, aug_s, first_head):
        for t in range(hps):
            xh = res[:, t * HEAD_DIM:(t + 1) * HEAD_DIM]
            xn = xh * _rms_scale(xh, HEAD_DIM) * gain
            out_ref[:, t * QK_WIDTH:t * QK_WIDTH + HEAD_DIM] = xn.astype(BF16)
            out_ref[:, t * QK_WIDTH + HEAD_DIM:(t + 1) * QK_WIDTH] = aug_s[first_head + t]

    @pl.when(j < per_kind)
    def _():
        qk_norm_store(q_ref, qg_ref[...] * (HEAD_DIM ** -0.5), augq_s, j * hps)

    @pl.when(jnp.logical_and(j >= per_kind, j < 2 * per_kind))
    def _():
        qk_norm_store(k_ref, kg_ref[...], augk_s, (j - per_kind) * hps)

    @pl.when(jnp.logical_and(j >= 2 * per_kind, j < 3 * per_kind))
    def _():
        for t in range(hps):
            vt_ref[t, 0] = res[:, t * HEAD_DIM:(t + 1) * HEAD_DIM].T.astype(BF16)

    @pl.when(j >= 3 * per_kind)
    def _():
        z_ref[...] = res


def _fox_proj(x, norm_g, w_in, f_bias, q_norm_g, k_norm_g, n_heads):
    s, d = x.shape
    bw = n_heads * HEAD_DIM
    tm, nb = B_TM, B_NB
    per_kind = bw // nb
    hps = nb // HEAD_DIM
    w_b = w_in[:, :4 * bw].astype(BF16)
    w_f = jnp.pad(w_in[:, 4 * bw:], ((0, 0), (0, LANES - n_heads))).astype(BF16)
    fb = jnp.pad(f_bias, (0, LANES - n_heads)).reshape(1, LANES)

    def kind(j, k):
        return jnp.clip(j - k * per_kind, 0, per_kind - 1)

    return pl.pallas_call(
        lambda *refs: _fox_proj_kernel(*refs, n_heads=n_heads, per_kind=per_kind),
        grid=(s // tm, 4 * per_kind),
        in_specs=[
            pl.BlockSpec((tm, d), lambda i, j: (i, 0)),
            pl.BlockSpec((1, d), lambda i, j: (0, 0)),
            pl.BlockSpec((d, LANES), lambda i, j: (0, 0)),
            pl.BlockSpec((1, LANES), lambda i, j: (0, 0)),
            pl.BlockSpec((d, nb), lambda i, j: (0, j)),
            pl.BlockSpec((1, HEAD_DIM), lambda i, j: (0, 0)),
            pl.BlockSpec((1, HEAD_DIM), lambda i, j: (0, 0)),
        ],
        out_specs=[
            pl.BlockSpec((tm, hps * QK_WIDTH), lambda i, j: (i, kind(j, 0))),
            pl.BlockSpec((tm, hps * QK_WIDTH), lambda i, j: (i, kind(j, 1))),
            pl.BlockSpec((hps, 1, HEAD_DIM, tm), lambda i, j: (kind(j, 2), i, 0, 0)),
            pl.BlockSpec((tm, nb), lambda i, j: (i, kind(j, 3))),
        ],
        out_shape=[
            jax.ShapeDtypeStruct((s, n_heads * QK_WIDTH), BF16),
            jax.ShapeDtypeStruct((s, n_heads * QK_WIDTH), BF16),
            jax.ShapeDtypeStruct((n_heads, s // tm, HEAD_DIM, tm), BF16),
            jax.ShapeDtypeStruct((s, bw), F32),
        ],
        scratch_shapes=[
            pltpu.VMEM((tm, d), BF16),
            pltpu.VMEM((n_heads, tm, LANES), BF16),
            pltpu.VMEM((n_heads, tm, LANES), BF16),
            pltpu.VMEM((1, LANES), F32),
        ],
        compiler_params=pltpu.CompilerParams(
            dimension_semantics=("arbitrary", "arbitrary"), vmem_limit_bytes=VMEM_LIMIT),
        name="fox_proj",
    )(x, norm_g.reshape(1, d), w_f, fb, w_b, q_norm_g.reshape(1, HEAD_DIM),
      k_norm_g.reshape(1, HEAD_DIM))


def _fox_attn_kernel(q_ref, k_ref, vt_ref, z_ref, y_ref, m_s, l_s, acc_s):
    i = pl.program_id(1)
    tq = q_ref.shape[0]
    tk = vt_ref.shape[3]
    q = q_ref[...]
    m_s[...] = jnp.full_like(m_s, -jnp.inf)
    l_s[...] = jnp.zeros_like(l_s)
    acc_s[...] = jnp.zeros_like(acc_s)

    def step(j, masked):
        kblk = k_ref[pl.ds(pl.multiple_of(j * tk, tk), tk), :]
        s = lax.dot_general(kblk, q, (((1,), (1,)), ((), ())), preferred_element_type=F32)
        if masked:
            row = lax.broadcasted_iota(jnp.int32, (tk, tq), 0)
            col = lax.broadcasted_iota(jnp.int32, (tk, tq), 1)
            s = jnp.where(row <= col, s, -jnp.inf)
        m_prev = m_s[...]
        m_new = jnp.maximum(m_prev, jnp.max(s, axis=0, keepdims=True))
        alpha = jnp.exp(m_prev - m_new)
        p = jnp.exp(s - m_new)
        l_s[...] = alpha * l_s[...] + jnp.sum(p, axis=0, keepdims=True)
        acc_s[...] = alpha * acc_s[...] + jnp.dot(vt_ref[0, j], p.astype(BF16),
                                                  preferred_element_type=F32)
        m_s[...] = m_new

    def body(j, carry):
        step(j, False)
        return carry

    lax.fori_loop(0, i, body, 0)
    step(i, True)
    o = (acc_s[...] / l_s[...]).T
    y_ref[...] = (o * _silu(z_ref[...])).astype(BF16)


def _fox_attn(qp, kp, vt, z, n_heads):
    s = qp.shape[0]
    tq = B_TQ
    nk, tk = vt.shape[1], vt.shape[3]
    assert tq == tk
    return pl.pallas_call(
        _fox_attn_kernel,
        grid=(n_heads, s // tq),
        in_specs=[
            pl.BlockSpec((tq, QK_WIDTH), lambda h, i: (i, h)),
            pl.BlockSpec((s, QK_WIDTH), lambda h, i: (0, h)),
            pl.BlockSpec((1, nk, HEAD_DIM, tk), lambda h, i: (h, 0, 0, 0)),
            pl.BlockSpec((tq, HEAD_DIM), lambda h, i: (i, h)),
        ],
        out_specs=pl.BlockSpec((tq, HEAD_DIM), lambda h, i: (i, h)),
        out_shape=jax.ShapeDtypeStruct((s, n_heads * HEAD_DIM), BF16),
        scratch_shapes=[
            pltpu.VMEM((1, tq), F32),
            pltpu.VMEM((1, tq), F32),
            pltpu.VMEM((HEAD_DIM, tq), F32),
        ],
        compiler_params=pltpu.CompilerParams(
            dimension_semantics=("arbitrary", "arbitrary"), vmem_limit_bytes=VMEM_LIMIT),
        name="fox_attn",
    )(qp, kp, vt, z)


def _out_proj_kernel(y_ref, w_ref, x_ref, o_ref):
    o_ref[...] = x_ref[...] + jnp.dot(y_ref[...], w_ref[...], preferred_element_type=F32)


def _out_proj(y, w_out, x):
    s, d = x.shape
    kdim = y.shape[1]
    tm = O_TM
    return pl.pallas_call(
        _out_proj_kernel,
        grid=(s // tm,),
        in_specs=[
            pl.BlockSpec((tm, kdim), lambda i: (i, 0)),
            pl.BlockSpec((kdim, d), lambda i: (0, 0)),
            pl.BlockSpec((tm, d), lambda i: (i, 0)),
        ],
        out_specs=pl.BlockSpec((tm, d), lambda i: (i, 0)),
        out_shape=jax.ShapeDtypeStruct((s, d), F32),
        compiler_params=pltpu.CompilerParams(
            dimension_semantics=("arbitrary",), vmem_limit_bytes=VMEM_LIMIT),
        name="fox_out_proj",
    )(y, w_out.astype(BF16), x)


def _fox_layer(x, norm_g, w_in, f_bias, q_norm_g, k_norm_g, w_out):
    n_heads = f_bias.shape[0]
    qp, kp, vt, z = _fox_proj(x, norm_g, w_in, f_bias, q_norm_g, k_norm_g, n_heads)
    y = _fox_attn(qp, kp, vt, z, n_heads)
    return _out_proj(y, w_out, x)


def kernel(x, a_norm_g, a_w_in, a_v_norm_g, a_w_s, a_b_s, a_w_out,
           b_norm_g, b_w_in, b_f_bias, b_q_norm_g, b_k_norm_g, b_w_out):
    b, s, d = x.shape
    assert b == 1
    h = x.reshape(s, d)
    depth = a_norm_g.shape[0] + b_norm_g.shape[0]
    for layer in range(depth):
        j = layer // 2
        if layer % 2 == 0:
            h = _gmlp_layer(h, a_norm_g[j], a_w_in[j], a_v_norm_g[j], a_w_s[j], a_b_s[j], a_w_out[j])
        else:
            h = _fox_layer(h, b_norm_g[j], b_w_in[j], b_f_bias[j], b_q_norm_g[j], b_k_norm_g[j],
                           b_w_out[j])
    return h.reshape(b, s, d)
```

```python
import math

import jax
import jax.numpy as jnp
from jax import lax
from jax.experimental import pallas as pl
from jax.experimental.pallas import tpu as pltpu

F32 = jnp.float32
BF16 = jnp.bfloat16

EPS = 1e-6
LANES = 128
CHUNK = 128
N_GROUPS = 16
HEAD_DIM = 128
QK_WIDTH = 2 * HEAD_DIM
VMEM_LIMIT = 56 * 1024 * 1024

A_TM = 512
A_BV = 1024
A_GS = 2

B_TM = 512
B_NB = 1024
B_TQ = 512
KV_WIDE = 4
O_TM = 512


def _gelu_tanh(x):
    c = math.sqrt(2.0 / math.pi)
    return x * (0.5 * (1.0 + jnp.tanh(c * (x + 0.044715 * (x * x * x)))))


def _silu(x):
    return x / (1.0 + jnp.exp(-x))


def _log_sigmoid(x):
    return -(jnp.maximum(-x, 0.0) + jnp.log1p(jnp.exp(-jnp.abs(x))))


def _rms_scale(x, width):
    return lax.rsqrt(jnp.sum(x * x, axis=-1, keepdims=True) * (1.0 / width) + EPS)


def _split3(x):
    a1 = x.astype(BF16)
    r1 = x - a1.astype(F32)
    a2 = r1.astype(BF16)
    a3 = (r1 - a2.astype(F32)).astype(BF16)
    return a1, a2, a3


def _gmlp_kernel(x_ref, ng_ref, wv_ref, wu_ref, wz_ref, vg_ref, ws_ref, bs_ref, wo_ref,
                 o_ref, h_s, gv_s, ssq_s, r_s, *, nv, width):
    j = pl.program_id(1)
    tm = x_ref.shape[0]
    gdim = gv_s.shape[2]

    @pl.when(j == 0)
    def _():
        x = x_ref[...]
        h_s[...] = (x * _rms_scale(x, x.shape[1]) * ng_ref[...]).astype(BF16)
        ssq_s[...] = jnp.zeros_like(ssq_s)

    @pl.when(j < nv)
    def _():
        v = jnp.dot(h_s[...], wv_ref[...], preferred_element_type=F32)
        gv = _gelu_tanh(v)
        sq = gv * gv
        part = sq[:, 0:LANES]
        for c in range(1, gv.shape[1] // LANES):
            part = part + sq[:, c * LANES:(c + 1) * LANES]
        ssq_s[...] += part
        per_step = gv.shape[1] // gdim
        for t in range(per_step):
            gv_s[j * per_step + t] = gv[:, t * gdim:(t + 1) * gdim].astype(BF16)

    @pl.when(j >= nv)
    def _():
        jj = j - nv

        @pl.when(jj == 0)
        def _():
            r_s[...] = lax.rsqrt(jnp.sum(ssq_s[...], axis=1, keepdims=True) * (1.0 / width) + EPS)

        h = h_s[...]
        u = jnp.dot(h, wu_ref[...], preferred_element_type=F32)
        z = jnp.dot(h, wz_ref[...], preferred_element_type=F32)
        r = r_s[...]
        gs = ws_ref.shape[0]
        row = lax.broadcasted_iota(jnp.int32, (CHUNK, CHUNK), 0)
        col = lax.broadcasted_iota(jnp.int32, (CHUNK, CHUNK), 1)
        causal = row >= col
        ys = []
        for t in range(gs):
            sl = slice(t * gdim, (t + 1) * gdim)
            vn = (gv_s[jj * gs + t].astype(F32) * r * vg_ref[:, sl]).astype(BF16)
            w = jnp.where(causal, ws_ref[t], 0.0).astype(BF16)
            b = bs_ref[t]
            mixed = jnp.concatenate(
                [jnp.dot(w, vn[c * CHUNK:(c + 1) * CHUNK], preferred_element_type=F32) + b
                 for c in range(tm // CHUNK)], axis=0)
            ys.append((_gelu_tanh(u[:, sl]) * mixed * _silu(z[:, sl])).astype(BF16))
        y = jnp.concatenate(ys, axis=1)
        contrib = jnp.dot(y, wo_ref[...], preferred_element_type=F32)

        @pl.when(jj == 0)
        def _():
            o_ref[...] = x_ref[...] + contrib

        @pl.when(jj > 0)
        def _():
            o_ref[...] += contrib


def _gmlp_layer(x, norm_g, w_in, v_norm_g, w_s, b_s, w_out):
    s, d = x.shape
    width = w_out.shape[0]
    gdim = width // N_GROUPS
    tm, bv, gs = A_TM, A_BV, A_GS
    bu = gs * gdim
    nv, nu = width // bv, N_GROUPS // gs

    def uz(j):
        return jnp.maximum(j - nv, 0)

    in_specs = [
        pl.BlockSpec((tm, d), lambda i, j: (i, 0)),
        pl.BlockSpec((1, d), lambda i, j: (0, 0)),
        pl.BlockSpec((d, bv), lambda i, j: (0, width // bv + jnp.minimum(j, nv - 1))),
        pl.BlockSpec((d, bu), lambda i, j: (0, uz(j))),
        pl.BlockSpec((d, bu), lambda i, j: (0, 2 * width // bu + uz(j))),
        pl.BlockSpec((1, bu), lambda i, j: (0, uz(j))),
        pl.BlockSpec((gs, CHUNK, CHUNK), lambda i, j: (uz(j), 0, 0)),
        pl.BlockSpec((gs, CHUNK, 1), lambda i, j: (uz(j), 0, 0)),
        pl.BlockSpec((bu, d), lambda i, j: (uz(j), 0)),
    ]
    w_in_b = w_in.astype(BF16)
    return pl.pallas_call(
        lambda *refs: _gmlp_kernel(*refs, nv=nv, width=width),
        grid=(s // tm, nv + nu),
        in_specs=in_specs,
        out_specs=pl.BlockSpec((tm, d), lambda i, j: (i, 0)),
        out_shape=jax.ShapeDtypeStruct((s, d), F32),
        scratch_shapes=[
            pltpu.VMEM((tm, d), BF16),
            pltpu.VMEM((N_GROUPS, tm, gdim), BF16),
            pltpu.VMEM((tm, LANES), F32),
            pltpu.VMEM((tm, 1), F32),
        ],
        compiler_params=pltpu.CompilerParams(
            dimension_semantics=("arbitrary", "arbitrary"), vmem_limit_bytes=VMEM_LIMIT),
        name="gmlp_layer",
    )(x, norm_g.reshape(1, d), w_in_b, w_in_b, w_in_b, v_norm_g.reshape(1, width),
      w_s, b_s.reshape(N_GROUPS, CHUNK, 1), w_out.astype(BF16))


def _fox_proj_kernel(x_ref, ng_ref, wf_ref, fb_ref, w_ref, qg_ref, kg_ref,
                     q_ref, k_ref, vt_ref, z_ref, h_s, augq_s, augk_s, carry_s, *, n_heads, per_kind):
    i = pl.program_id(0)
    j = pl.program_id(1)
    tm = x_ref.shape[0]
    hps = w_ref.shape[1] // HEAD_DIM

    @pl.when(j == 0)
    def _():
        @pl.when(i == 0)
        def _():
            carry_s[...] = jnp.zeros_like(carry_s)

        x = x_ref[...]
        h = (x * _rms_scale(x, x.shape[1]) * ng_ref[...]).astype(BF16)
        h_s[...] = h
        logit = jnp.dot(h, wf_ref[...], preferred_element_type=F32) + fb_ref[...]
        a1, a2, a3 = _split3(_log_sigmoid(logit))
        row = lax.broadcasted_iota(jnp.int32, (tm, tm), 0)
        col = lax.broadcasted_iota(jnp.int32, (tm, tm), 1)
        tri = jnp.where(row >= col, 1.0, 0.0).astype(BF16)
        cum = (jnp.dot(tri, a1, preferred_element_type=F32)
               + jnp.dot(tri, a2, preferred_element_type=F32)
               + jnp.dot(tri, a3, preferred_element_type=F32)) + carry_s[...]
        carry_s[...] = cum[tm - 1:tm, :]
        lane = lax.broadcasted_iota(jnp.int32, (tm, LANES), 1)
        one = jnp.ones((tm, LANES), F32)
        zero = jnp.zeros((tm, LANES), F32)
        for hh in range(n_heads):
            c1, c2, c3 = _split3(jnp.broadcast_to(cum[:, hh:hh + 1], (tm, LANES)))
            c1, c2, c3 = c1.astype(F32), c2.astype(F32), c3.astype(F32)
            augq = jnp.where(lane == 0, c1, jnp.where(lane == 1, c2, jnp.where(
                lane == 2, c3, jnp.where(lane < 6, one, zero))))
            augk = jnp.where(lane < 3, one, jnp.where(lane == 3, -c1, jnp.where(
                lane == 4, -c2, jnp.where(lane == 5, -c3, jnp.where(lane < 9, one, zero)))))
            augq_s[hh] = augq.astype(BF16)
            augk_s[hh] = augk.astype(BF16)

    res = jnp.dot(h_s[...], w_ref[...], preferred_element_type=F32)

    def qk_norm_store(out_ref, gain, aug_s, first_head, key_norm_bound):
        for t in range(hps):
            xh = res[:, t * HEAD_DIM:(t + 1) * HEAD_DIM]
            xn = xh * _rms_scale(xh, HEAD_DIM) * gain
            aug = aug_s[first_head + t]
            if key_norm_bound is not None:
                shift = jnp.sqrt(jnp.sum(xn * xn, axis=1, keepdims=True)) * key_norm_bound
                m1, m2, m3 = _split3(jnp.broadcast_to(-shift, (tm, LANES)))
                lane = lax.broadcasted_iota(jnp.int32, (tm, LANES), 1)
                aug = jnp.where(lane == 6, m1, jnp.where(lane == 7, m2, jnp.where(lane == 8, m3, aug)))
            out_ref[:, t * QK_WIDTH:t * QK_WIDTH + HEAD_DIM] = xn.astype(BF16)
            out_ref[:, t * QK_WIDTH + HEAD_DIM:(t + 1) * QK_WIDTH] = aug

    @pl.when(j < per_kind)
    def _():
        kg = kg_ref[...]
        key_norm_bound = jnp.sqrt(HEAD_DIM * jnp.max(kg * kg, axis=1, keepdims=True))
        qk_norm_store(q_ref, qg_ref[...] * (HEAD_DIM ** -0.5), augq_s, j * hps, key_norm_bound)

    @pl.when(jnp.logical_and(j >= per_kind, j < 2 * per_kind))
    def _():
        qk_norm_store(k_ref, kg_ref[...], augk_s, (j - per_kind) * hps, None)

    @pl.when(jnp.logical_and(j >= 2 * per_kind, j < 3 * per_kind))
    def _():
        for t in range(hps):
            vt_ref[t, 0] = res[:, t * HEAD_DIM:(t + 1) * HEAD_DIM].T.astype(BF16)

    @pl.when(j >= 3 * per_kind)
    def _():
        z_ref[...] = res


def _fox_proj(x, norm_g, w_in, f_bias, q_norm_g, k_norm_g, n_heads):
    s, d = x.shape
    bw = n_heads * HEAD_DIM
    tm, nb = B_TM, B_NB
    per_kind = bw // nb
    hps = nb // HEAD_DIM
    w_b = w_in.astype(BF16)
    w_f = jnp.pad(w_in[:, 4 * bw:], ((0, 0), (0, LANES - n_heads))).astype(BF16)
    fb = jnp.pad(f_bias, (0, LANES - n_heads)).reshape(1, LANES)

    def kind(j, k):
        return jnp.clip(j - k * per_kind, 0, per_kind - 1)

    return pl.pallas_call(
        lambda *refs: _fox_proj_kernel(*refs, n_heads=n_heads, per_kind=per_kind),
        grid=(s // tm, 4 * per_kind),
        in_specs=[
            pl.BlockSpec((tm, d), lambda i, j: (i, 0)),
            pl.BlockSpec((1, d), lambda i, j: (0, 0)),
            pl.BlockSpec((d, LANES), lambda i, j: (0, 0)),
            pl.BlockSpec((1, LANES), lambda i, j: (0, 0)),
            pl.BlockSpec((d, nb), lambda i, j: (0, j)),
            pl.BlockSpec((1, HEAD_DIM), lambda i, j: (0, 0)),
            pl.BlockSpec((1, HEAD_DIM), lambda i, j: (0, 0)),
        ],
        out_specs=[
            pl.BlockSpec((tm, hps * QK_WIDTH), lambda i, j: (i, kind(j, 0))),
            pl.BlockSpec((tm, hps * QK_WIDTH), lambda i, j: (i, kind(j, 1))),
            pl.BlockSpec((hps, 1, HEAD_DIM, tm), lambda i, j: (kind(j, 2), i, 0, 0)),
            pl.BlockSpec((tm, nb), lambda i, j: (i, kind(j, 3))),
        ],
        out_shape=[
            jax.ShapeDtypeStruct((s, n_heads * QK_WIDTH), BF16),
            jax.ShapeDtypeStruct((s, n_heads * QK_WIDTH), BF16),
            jax.ShapeDtypeStruct((n_heads, s // tm, HEAD_DIM, tm), BF16),
            jax.ShapeDtypeStruct((s, bw), F32),
        ],
        scratch_shapes=[
            pltpu.VMEM((tm, d), BF16),
            pltpu.VMEM((n_heads, tm, LANES), BF16),
            pltpu.VMEM((n_heads, tm, LANES), BF16),
            pltpu.VMEM((1, LANES), F32),
        ],
        compiler_params=pltpu.CompilerParams(
            dimension_semantics=("arbitrary", "arbitrary"), vmem_limit_bytes=VMEM_LIMIT),
        name="fox_proj",
    )(x, norm_g.reshape(1, d), w_f, fb, w_b, q_norm_g.reshape(1, HEAD_DIM),
      k_norm_g.reshape(1, HEAD_DIM))


def _fox_attn_kernel(q_ref, k_ref, vt_ref, z_ref, y_ref, m_s, l_s, acc_s):
    i = pl.program_id(1)
    tq = q_ref.shape[0]
    tk = vt_ref.shape[3]
    q = q_ref[...]

    def scores(j, masked, ntile=1):
        kblk = k_ref[pl.ds(pl.multiple_of(j * tk, tk), ntile * tk), :]
        s = lax.dot_general(kblk, q, (((1,), (1,)), ((), ())), preferred_element_type=F32)
        if masked:
            row = lax.broadcasted_iota(jnp.int32, (ntile * tk, tq), 0)
            col = lax.broadcasted_iota(jnp.int32, (ntile * tk, tq), 1)
            s = jnp.where(row <= col + (ntile - 1) * tk, s, -jnp.inf)
        return s

    def values_t(j, ntile=1):
        if ntile == 1:
            return vt_ref[0, j]
        return jnp.concatenate([vt_ref[0, j + t] for t in range(ntile)], axis=1)

    def finish(l_row):
        o = (acc_s[...] / l_row).T
        y_ref[...] = (o * _silu(z_ref[...])).astype(BF16)

    l_s[...] = jnp.zeros_like(l_s)
    acc_s[...] = jnp.zeros_like(acc_s)

    def fast_step(j, masked, ntile=1):
        p = jnp.exp(scores(j, masked, ntile))
        l_s[...] += jnp.sum(p.reshape(ntile * tk // 8, 8, tq), axis=0)
        acc_s[...] += jnp.dot(values_t(j, ntile), p.astype(BF16), preferred_element_type=F32)

    def fast_body_wide(jw, carry):
        fast_step(jw * KV_WIDE, False, KV_WIDE)
        return carry

    n_wide = i // KV_WIDE
    lax.fori_loop(0, n_wide, fast_body_wide, 0)
    for tail in range(KV_WIDE):
        @pl.when(i - n_wide * KV_WIDE == tail)
        def _():
            fast_step(n_wide * KV_WIDE, True, tail + 1)
    l_fast = jnp.sum(l_s[...], axis=0, keepdims=True)
    safe = jnp.logical_and(jnp.min(l_fast) > 2.0 ** -100, jnp.max(l_fast) < 2.0 ** 100)

    @pl.when(safe)
    def _():
        finish(l_fast)

    @pl.when(jnp.logical_not(safe))
    def _():
        m_s[...] = jnp.full_like(m_s, -jnp.inf)
        l_s[...] = jnp.zeros_like(l_s)
        acc_s[...] = jnp.zeros_like(acc_s)

        def slow_step(j, masked):
            s = scores(j, masked)
            m_prev = m_s[...]
            m_new = jnp.maximum(m_prev, jnp.max(s, axis=0, keepdims=True))
            alpha = jnp.exp(m_prev - m_new)
            p = jnp.exp(s - m_new)
            l_s[0:1, :] = alpha * l_s[0:1, :] + jnp.sum(p, axis=0, keepdims=True)
            acc_s[...] = alpha * acc_s[...] + jnp.dot(vt_ref[0, j], p.astype(BF16),
                                                      preferred_element_type=F32)
            m_s[...] = m_new

        def slow_body(j, carry):
            slow_step(j, False)
            return carry

        lax.fori_loop(0, i, slow_body, 0)
        slow_step(i, True)
        finish(l_s[0:1, :])


def _fox_attn(qp, kp, vt, z, n_heads):
    s = qp.shape[0]
    tq = B_TQ
    nk, tk = vt.shape[1], vt.shape[3]
    assert tq == tk
    return pl.pallas_call(
        _fox_attn_kernel,
        grid=(n_heads, s // tq),
        in_specs=[
            pl.BlockSpec((tq, QK_WIDTH), lambda h, i: (i, h)),
            pl.BlockSpec((s, QK_WIDTH), lambda h, i: (0, h)),
            pl.BlockSpec((1, nk, HEAD_DIM, tk), lambda h, i: (h, 0, 0, 0)),
            pl.BlockSpec((tq, HEAD_DIM), lambda h, i: (i, h)),
        ],
        out_specs=pl.BlockSpec((tq, HEAD_DIM), lambda h, i: (i, h)),
        out_shape=jax.ShapeDtypeStruct((s, n_heads * HEAD_DIM), BF16),
        scratch_shapes=[
            pltpu.VMEM((1, tq), F32),
            pltpu.VMEM((8, tq), F32),
            pltpu.VMEM((HEAD_DIM, tq), F32),
        ],
        compiler_params=pltpu.CompilerParams(
            dimension_semantics=("arbitrary", "arbitrary"), vmem_limit_bytes=VMEM_LIMIT),
        name="fox_attn",
    )(qp, kp, vt, z)


def _out_proj_kernel(y_ref, w_ref, x_ref, o_ref):
    o_ref[...] = x_ref[...] + jnp.dot(y_ref[...], w_ref[...], preferred_element_type=F32)


def _out_proj(y, w_out, x):
    s, d = x.shape
    kdim = y.shape[1]
    tm = O_TM
    return pl.pallas_call(
        _out_proj_kernel,
        grid=(s // tm,),
        in_specs=[
            pl.BlockSpec((tm, kdim), lambda i: (i, 0)),
            pl.BlockSpec((kdim, d), lambda i: (0, 0)),
            pl.BlockSpec((tm, d), lambda i: (i, 0)),
        ],
        out_specs=pl.BlockSpec((tm, d), lambda i: (i, 0)),
        out_shape=jax.ShapeDtypeStruct((s, d), F32),
        compiler_params=pltpu.CompilerParams(
            dimension_semantics=("arbitrary",), vmem_limit_bytes=VMEM_LIMIT),
        name="fox_out_proj",
    )(y, w_out.astype(BF16), x)


def _fox_layer(x, norm_g, w_in, f_bias, q_norm_g, k_norm_g, w_out):
    n_heads = f_bias.shape[0]
    qp, kp, vt, z = _fox_proj(x, norm_g, w_in, f_bias, q_norm_g, k_norm_g, n_heads)
    y = _fox_attn(qp, kp, vt, z, n_heads)
    return _out_proj(y, w_out, x)


def kernel(x, a_norm_g, a_w_in, a_v_norm_g, a_w_s, a_b_s, a_w_out,
           b_norm_g, b_w_in, b_f_bias, b_q_norm_g, b_k_norm_g, b_w_out):
    b, s, d = x.shape
    assert b == 1
    h = x.reshape(s, d)
    depth = a_norm_g.shape[0] + b_norm_g.shape[0]
    for layer in range(depth):
        j = layer // 2
        if layer % 2 == 0:
            h = _gmlp_layer(h, a_norm_g[j], a_w_in[j], a_v_norm_g[j], a_w_s[j], a_b_s[j], a_w_out[j])
        else:
            h = _fox_layer(h, b_norm_g[j], b_w_in[j], b_f_bias[j], b_q_norm_g[j], b_k_norm_g[j],
                           b_w_out[j])
    return h.reshape(b, s, d)
```
